```python
import math
import jax, jax.numpy as jnp
from jax import lax
import numpy as np

D_MODEL = 2048
BATCH = 2
SEQ = 16384
DEPTH = 4
DEC_BATCH = 2
DEC_SEQ = 8192
PAST_LEN = 128

HEAD_DIM = 64
A_HEADS = 12
A_KV_HEADS = 4
A_GROUP = A_HEADS // A_KV_HEADS
WINDOW = 128
BLOCK = 128
N_BUCKETS = 32
MAX_DIST = 128
B_GROUPS = 8
CHUNK = 128
C_HEADS = 12
GRID_W = 64
NA_ROWS_MAX = 8
NA_COLS = 16
NA_QC = 16
NA_KW = 32

A_WIDTH = A_HEADS * HEAD_DIM
B_WIDTH = B_GROUPS * HEAD_DIM
C_WIDTH = C_HEADS * HEAD_DIM
MIX_WIDTH = A_WIDTH + B_WIDTH + C_WIDTH
KV_WIDTH = A_KV_HEADS * HEAD_DIM
IN_SPLITS = (A_WIDTH, KV_WIDTH, KV_WIDTH, B_WIDTH, B_WIDTH, C_WIDTH, C_WIDTH, C_WIDTH)
IN_WIDTH = sum(IN_SPLITS)
D_FF = 5504
CONV_W = 3
EPS = 1e-6
NEG = -1e30

kernel_name = "hymba_style_window_gmlp_natten_encoder"


def rms_norm(x, g):
    xf = x.astype(jnp.float32)
    y = xf * lax.rsqrt(jnp.mean(xf * xf, axis=-1, keepdims=True) + EPS) * g.astype(jnp.float32)
    return y.astype(x.dtype)


def layer_norm(x, g, b):
    xf = x.astype(jnp.float32)
    mu = jnp.mean(xf, axis=-1, keepdims=True)
    var = jnp.mean(jnp.square(xf - mu), axis=-1, keepdims=True)
    y = (xf - mu) * lax.rsqrt(var + EPS) * g.astype(jnp.float32) + b.astype(jnp.float32)
    return y.astype(x.dtype)


def t5_bucket(rel):
    nb = N_BUCKETS // 2
    max_exact = nb // 2
    ret = (rel > 0).astype(jnp.int32) * nb
    n = jnp.abs(rel)
    nf = jnp.maximum(n, 1).astype(jnp.float32)
    large = max_exact + (jnp.log(nf / max_exact) / math.log(MAX_DIST / max_exact) * (nb - max_exact)).astype(jnp.int32)
    large = jnp.minimum(large, nb - 1)
    return ret + jnp.where(n < max_exact, n, large)


def window_attention(q, k, v, sink, rel_bias):
    bsz, t = q.shape[0], q.shape[1]
    nb = t // BLOCK
    s_len = 3 * BLOCK
    qb = q.reshape(bsz, nb, BLOCK, A_KV_HEADS, A_GROUP, HEAD_DIM)

    def band(z):
        zp = jnp.pad(z, ((0, 0), (BLOCK, BLOCK), (0, 0), (0, 0))).reshape(bsz, nb + 2, BLOCK, A_KV_HEADS, HEAD_DIM)
        return jnp.concatenate([zp[:, :-2], zp[:, 1:-1], zp[:, 2:]], axis=2)

    kb, vb = band(k), band(v)
    scale = HEAD_DIM ** -0.5
    s = jnp.einsum('bnqkgd,bnskd->bnkgqs', qb, kb).astype(jnp.float32) * scale
    rel = jnp.arange(s_len)[None, :] - BLOCK - jnp.arange(BLOCK)[:, None]
    bias = rel_bias.astype(jnp.float32)[t5_bucket(rel)]
    bias = bias.transpose(2, 0, 1).reshape(A_KV_HEADS, A_GROUP, BLOCK, s_len)
    kpos = jnp.arange(nb)[:, None] * BLOCK + jnp.arange(s_len)[None, :] - BLOCK
    valid = (jnp.abs(rel) <= WINDOW)[None] & ((kpos >= 0) & (kpos < t))[:, None, :]
    s = jnp.where(valid[None, :, None, None], s + bias[None, None], NEG)
    sink_l = sink.astype(jnp.float32).reshape(A_KV_HEADS, A_GROUP)[None, None, :, :, None, None]
    m = jnp.maximum(jnp.max(s, axis=-1, keepdims=True), sink_l)
    p = jnp.exp(s - m)
    p = (p / (jnp.sum(p, axis=-1, keepdims=True) + jnp.exp(sink_l - m))).astype(v.dtype)
    o = jnp.einsum('bnkgqs,bnskd->bnqkgd', p, vb)
    return o.reshape(bsz, t, A_WIDTH)


def spatial_gating(u, v, ln_g, ln_b, w_s, b_s):
    bsz, t = u.shape[0], u.shape[1]
    nc = t // CHUNK
    u = jax.nn.gelu(u)
    v = layer_norm(jax.nn.gelu(v), ln_g, ln_b)
    vc = v.reshape(bsz, nc, CHUNK, B_GROUPS, HEAD_DIM)
    s = jnp.einsum('gpq,bcqgd->bcpgd', w_s, vc) + b_s.T[None, None, :, :, None]
    return u * s.reshape(bsz, t, B_WIDTH)


def neighbourhood_attention(q, k, v, rpb):
    bsz, t = q.shape[0], q.shape[1]
    rows = t // GRID_W
    kr = min(NA_ROWS_MAX, rows)
    n_cb = GRID_W // NA_QC
    r = jnp.arange(rows)
    row_start = jnp.clip(r - kr // 2, 0, rows - kr)
    key_rows = row_start[:, None] + jnp.arange(kr)[None, :]
    qcol = jnp.arange(GRID_W).reshape(n_cb, NA_QC)
    col_start = jnp.clip(qcol - NA_COLS // 2, 0, GRID_W - NA_COLS)
    win_start = jnp.clip(qcol[:, 0] - NA_COLS // 2, 0, GRID_W - NA_KW)
    key_cols = win_start[:, None] + jnp.arange(NA_KW)[None, :]
    flat = key_rows[:, None, :, None] * GRID_W + key_cols[None, :, None, :]
    kg = k[:, flat]
    vg = v[:, flat]
    qg = q.reshape(bsz, rows, n_cb, NA_QC, C_HEADS, HEAD_DIM)
    scale = HEAD_DIM ** -0.5
    s = jnp.einsum('brcqhd,brcawhd->brchqaw', qg, kg).astype(jnp.float32) * scale
    dr = key_rows - r[:, None]
    dc = key_cols[:, None, :] - qcol[:, :, None]
    kc = key_cols[:, None, :]
    colmask = (kc >= col_start[:, :, None]) & (kc < col_start[:, :, None] + NA_COLS)
    ri = dr + NA_ROWS_MAX - 1
    ci = jnp.clip(dc + NA_COLS - 1, 0, 2 * NA_COLS - 2)
    bias = rpb.astype(jnp.float32)[:, ri[:, None, None, :, None], ci[None, :, :, None, :]]
    bias = bias.transpose(1, 2, 0, 3, 4, 5)
    s = jnp.where(colmask[None, None, :, None, :, None, :], s + bias[None], NEG)
    p = jax.nn.softmax(s, axis=(-2, -1)).astype(v.dtype)
    o = jnp.einsum('brchqaw,brcawhd->brcqhd', p, vg)
    return o.reshape(bsz, t, C_WIDTH)


def conv_ffn(x, w_up, conv_w, conv_b, w_down):
    h = x @ w_up
    hp = jnp.pad(h, ((0, 0), (1, 1), (0, 0)))
    h = hp[:, :-2] * conv_w[0] + hp[:, 1:-1] * conv_w[1] + hp[:, 2:] * conv_w[2] + conv_b
    g, val = jnp.split(h, 2, axis=-1)
    return (jax.nn.gelu(g) * val) @ w_down


def run_trunk(x, rel_bias, norm1_g, w_in, sink, sgu_ln_g, sgu_ln_b, w_spatial, b_spatial,
              na_rpb, gn_a, gn_b, gn_c, w_out, norm2_g, w_up, conv_w, conv_b, w_down, final_g):
    bsz, t = x.shape[0], x.shape[1]
    offs = [int(o) for o in np.cumsum(IN_SPLITS)[:-1]]
    h = x
    for l in range(DEPTH):
        xn = rms_norm(h, norm1_g[l])
        proj = xn @ w_in[l]
        qa, ka, va, ub, vb, qc, kc, vc = jnp.split(proj, offs, axis=-1)
        ya = window_attention(qa.reshape(bsz, t, A_HEADS, HEAD_DIM),
                              ka.reshape(bsz, t, A_KV_HEADS, HEAD_DIM),
                              va.reshape(bsz, t, A_KV_HEADS, HEAD_DIM), sink[l], rel_bias)
        yb = spatial_gating(ub, vb, sgu_ln_g[l], sgu_ln_b[l], w_spatial[l], b_spatial[l])
        yc = neighbourhood_attention(qc.reshape(bsz, t, C_HEADS, HEAD_DIM),
                                     kc.reshape(bsz, t, C_HEADS, HEAD_DIM),
                                     vc.reshape(bsz, t, C_HEADS, HEAD_DIM), na_rpb[l])
        merged = jnp.concatenate([rms_norm(ya, gn_a[l]), rms_norm(yb, gn_b[l]), rms_norm(yc, gn_c[l])], axis=-1)
        h = h + merged @ w_out[l]
        h = h + conv_ffn(rms_norm(h, norm2_g[l]), w_up[l], conv_w[l], conv_b[l], w_down[l])
    return rms_norm(h, final_g)


def setup_inputs(seed: int = 0) -> dict:
    key = jax.random.key(seed)
    ks = jax.random.split(key, 24)
    f32 = jnp.float32
    nrm = lambda k, shape: jax.random.normal(k, shape, f32)
    res_scale = (2.0 * DEPTH) ** -0.5
    return {
        "x_prompt": nrm(ks[0], (BATCH, SEQ, D_MODEL)),
        "x_sample": nrm(ks[1], (DEC_BATCH, DEC_SEQ, D_MODEL)),
        "rel_bias": 0.2 * nrm(ks[2], (N_BUCKETS, A_HEADS)),
        "norm1_g": 1.0 + 0.02 * nrm(ks[3], (DEPTH, D_MODEL)),
        "w_in": nrm(ks[4], (DEPTH, D_MODEL, IN_WIDTH)) * D_MODEL ** -0.5,
        "sink": 0.5 * nrm(ks[5], (DEPTH, A_HEADS)),
        "sgu_ln_g": 1.0 + 0.02 * nrm(ks[6], (DEPTH, B_WIDTH)),
        "sgu_ln_b": 0.02 * nrm(ks[7], (DEPTH, B_WIDTH)),
        "w_spatial": nrm(ks[8], (DEPTH, B_GROUPS, CHUNK, CHUNK)) * CHUNK ** -0.5,
        "b_spatial": 0.02 * nrm(ks[9], (DEPTH, B_GROUPS, CHUNK)),
        "na_rpb": 0.2 * nrm(ks[10], (DEPTH, C_HEADS, 2 * NA_ROWS_MAX - 1, 2 * NA_COLS - 1)),
        "gn_a": 1.0 + 0.02 * nrm(ks[11], (DEPTH, A_WIDTH)),
        "gn_b": 1.0 + 0.02 * nrm(ks[12], (DEPTH, B_WIDTH)),
        "gn_c": 1.0 + 0.02 * nrm(ks[13], (DEPTH, C_WIDTH)),
        "w_out": nrm(ks[14], (DEPTH, MIX_WIDTH, D_MODEL)) * (MIX_WIDTH ** -0.5) * res_scale,
        "norm2_g": 1.0 + 0.02 * nrm(ks[15], (DEPTH, D_MODEL)),
        "w_up": nrm(ks[16], (DEPTH, D_MODEL, 2 * D_FF)) * D_MODEL ** -0.5,
        "conv_w": nrm(ks[17], (DEPTH, CONV_W, 2 * D_FF)) * CONV_W ** -0.5,
        "conv_b": 0.02 * nrm(ks[18], (DEPTH, 2 * D_FF)),
        "w_down": nrm(ks[19], (DEPTH, D_FF, D_MODEL)) * (D_FF ** -0.5) * res_scale,
        "final_g": 1.0 + 0.02 * nrm(ks[20], (D_MODEL,)),
    }


def reference(x_prompt, x_sample, rel_bias, norm1_g, w_in, sink, sgu_ln_g, sgu_ln_b, w_spatial, b_spatial,
              na_rpb, gn_a, gn_b, gn_c, w_out, norm2_g, w_up, conv_w, conv_b, w_down, final_g):
    y_prompt = run_trunk(x_prompt, rel_bias, norm1_g, w_in, sink, sgu_ln_g, sgu_ln_b, w_spatial, b_spatial,
                         na_rpb, gn_a, gn_b, gn_c, w_out, norm2_g, w_up, conv_w, conv_b, w_down, final_g)
    y_sample = run_trunk(x_sample, rel_bias, norm1_g, w_in, sink, sgu_ln_g, sgu_ln_b, w_spatial, b_spatial,
                         na_rpb, gn_a, gn_b, gn_c, w_out, norm2_g, w_up, conv_w, conv_b, w_down, final_g)
    return (y_prompt, y_sample)
```

```python
import functools
import math

import jax
import jax.numpy as jnp
import numpy as np
from jax import lax
from jax.experimental import pallas as pl
from jax.experimental.pallas import tpu as pltpu

F32 = jnp.float32
BF16 = jnp.bfloat16

D_MODEL = 2048
DEPTH = 4
HEAD_DIM = 64
A_HEADS = 12
A_KV_HEADS = 4
A_GROUP = A_HEADS // A_KV_HEADS
WINDOW = 128
BLOCK = 128
N_BUCKETS = 32
MAX_DIST = 128
B_GROUPS = 8
CHUNK = 128
C_HEADS = 12
GRID_W = 64
NA_ROWS = 8
NA_COLS = 16
A_WIDTH = A_HEADS * HEAD_DIM
B_WIDTH = B_GROUPS * HEAD_DIM
C_WIDTH = C_HEADS * HEAD_DIM
KV_WIDTH = A_KV_HEADS * HEAD_DIM
IN_WIDTH = A_WIDTH + 2 * KV_WIDTH + 2 * B_WIDTH + 3 * C_WIDTH
D_FF = 5504
EPS = 1e-6
NEG = -1e30
SCALE = HEAD_DIM ** -0.5

LANES = 128
V7X_VMEM_BYTES = 64 * 1024 * 1024
V7X_VMEM_CAP = 56 * 1024 * 1024

QA_OFF = 0
QC_OFF = QA_OFF + A_WIDTH
KC_OFF = QC_OFF + C_WIDTH
VC_OFF = KC_OFF + C_WIDTH
UB_OFF = VC_OFF + C_WIDTH
VB_OFF = UB_OFF + B_WIDTH
KA_OFF = VB_OFF + B_WIDTH
VA_OFF = KA_OFF + KV_WIDTH

FF_TILE = 512
D_FF_PAD = -(-D_FF // FF_TILE) * FF_TILE
N_FF_TILES = D_FF_PAD // FF_TILE
HALO = 8


def _vmem_limit(block_bytes):
    return int(min(V7X_VMEM_CAP, max(32 * 1024 * 1024, 2 * block_bytes)))


def _gelu(x):
    c = math.sqrt(2.0 / math.pi)
    return 0.5 * x * (1.0 + jnp.tanh(c * (x + 0.044715 * (x * x * x))))


def _rms(x, g):
    ms = jnp.mean(x * x, axis=-1, keepdims=True)
    return x * lax.rsqrt(ms + EPS) * g


def _norm_proj_kernel(x_ref, g_ref, w_ref, o_ref, xn_ref, *, n_chunk):
    xn_ref[...] = _rms(x_ref[...], g_ref[...]).astype(BF16)
    for c in range(0, IN_WIDTH, n_chunk):
        o_ref[:, c:c + n_chunk] = jnp.dot(
            xn_ref[...], w_ref[:, c:c + n_chunk], preferred_element_type=F32).astype(BF16)


def _norm_proj(h, g, w, *, tm=512, n_chunk=512):
    m = h.shape[0]
    blocks = 2 * tm * D_MODEL * 4 + D_MODEL * IN_WIDTH * 2 + 2 * tm * IN_WIDTH * 2 + tm * D_MODEL * 2
    return pl.pallas_call(
        functools.partial(_norm_proj_kernel, n_chunk=n_chunk),
        grid=(m // tm,),
        in_specs=[
            pl.BlockSpec((tm, D_MODEL), lambda i: (i, 0)),
            pl.BlockSpec((1, D_MODEL), lambda i: (0, 0)),
            pl.BlockSpec((D_MODEL, IN_WIDTH), lambda i: (0, 0), pipeline_mode=pl.Buffered(1)),
        ],
        out_specs=pl.BlockSpec((tm, IN_WIDTH), lambda i: (i, 0)),
        out_shape=jax.ShapeDtypeStruct((m, IN_WIDTH), BF16),
        scratch_shapes=[pltpu.VMEM((tm, D_MODEL), BF16)],
        compiler_params=pltpu.CompilerParams(
            dimension_semantics=("arbitrary",), vmem_limit_bytes=_vmem_limit(blocks)),
        name="norm_proj",
    )(h, g, w)


def _window_attn_kernel(sink_ref, q_ref, k_ref, v_ref, kp_ref, kn_ref, vp_ref, vn_ref, bias_ref, o_ref,
                        kbuf, vbuf, *, tq, blocks_per_seq):
    i = pl.program_id(0)
    nb = tq // BLOCK
    kbuf[0:BLOCK] = kp_ref[...]
    kbuf[BLOCK:BLOCK + tq] = k_ref[...]
    kbuf[BLOCK + tq:] = kn_ref[...]
    vbuf[0:BLOCK] = vp_ref[...]
    vbuf[BLOCK:BLOCK + tq] = v_ref[...]
    vbuf[BLOCK + tq:] = vn_ref[...]
    col = lax.broadcasted_iota(jnp.int32, (1, 3 * BLOCK), 1)

    def body(j, carry):
        nseq = lax.rem(i * nb + j, blocks_per_seq)
        lo = jnp.where(nseq == 0, NEG, 0.0).astype(F32)
        hi = jnp.where(nseq == blocks_per_seq - 1, NEG, 0.0).astype(F32)
        pen = jnp.where(col < BLOCK, lo, jnp.where(col >= 2 * BLOCK, hi, 0.0))
        r0 = pl.multiple_of(j * BLOCK, BLOCK)
        q = q_ref[pl.ds(r0, BLOCK), :]
        kw = kbuf[pl.ds(r0, 3 * BLOCK), :]
        vw = vbuf[pl.ds(r0, 3 * BLOCK), :]
        outs = []
        for kh in range(A_KV_HEADS):
            k = kw[:, kh * HEAD_DIM:(kh + 1) * HEAD_DIM]
            v = vw[:, kh * HEAD_DIM:(kh + 1) * HEAD_DIM]
            for g in range(A_GROUP):
                h = kh * A_GROUP + g
                qh = q[:, h * HEAD_DIM:(h + 1) * HEAD_DIM] * SCALE
                s = lax.dot_general(qh, k, (((1,), (1,)), ((), ())), preferred_element_type=F32)
                s = s + bias_ref[h] + pen
                sink = sink_ref[h]
                m = jnp.maximum(jnp.max(s, axis=-1, keepdims=True), sink)
                p = jnp.exp(s - m)
                denom = jnp.sum(p, axis=-1, keepdims=True) + jnp.exp(sink - m)
                o = jnp.dot(p.astype(BF16), v, preferred_element_type=F32)
                outs.append(o * (1.0 / denom))
        o_ref[pl.ds(r0, BLOCK), :] = jnp.concatenate(outs, axis=-1)
        return carry

    lax.fori_loop(0, nb, body, 0)


def _window_attn(proj, bias, sink, seq_len, *, tq=512):
    m = proj.shape[0]
    nb = tq // BLOCK
    n_blocks = m // BLOCK
    ka, va = KA_OFF // KV_WIDTH, VA_OFF // KV_WIDTH
    blocks = 2 * (tq * A_WIDTH * 2 + 2 * (tq + 2 * BLOCK) * KV_WIDTH * 2 + tq * A_WIDTH * 4
                  + A_HEADS * BLOCK * 3 * BLOCK * 4) + 2 * (tq + 2 * BLOCK) * KV_WIDTH * 2
    prev_map = lambda c: (lambda i: (jnp.maximum(i * nb - 1, 0), c))
    next_map = lambda c: (lambda i: (jnp.minimum((i + 1) * nb, n_blocks - 1), c))
    return pl.pallas_call(
        functools.partial(_window_attn_kernel, tq=tq, blocks_per_seq=seq_len // BLOCK),
        grid=(m // tq,),
        in_specs=[
            pl.BlockSpec(memory_space=pltpu.SMEM),
            pl.BlockSpec((tq, A_WIDTH), lambda i: (i, QA_OFF // A_WIDTH)),
            pl.BlockSpec((tq, KV_WIDTH), lambda i: (i, ka)),
            pl.BlockSpec((tq, KV_WIDTH), lambda i: (i, va)),
            pl.BlockSpec((BLOCK, KV_WIDTH), prev_map(ka)),
            pl.BlockSpec((BLOCK, KV_WIDTH), next_map(ka)),
            pl.BlockSpec((BLOCK, KV_WIDTH), prev_map(va)),
            pl.BlockSpec((BLOCK, KV_WIDTH), next_map(va)),
            pl.BlockSpec((A_HEADS, BLOCK, 3 * BLOCK), lambda i: (0, 0, 0)),
        ],
        out_specs=pl.BlockSpec((tq, A_WIDTH), lambda i: (i, 0)),
        out_shape=jax.ShapeDtypeStruct((m, A_WIDTH), F32),
        scratch_shapes=[pltpu.VMEM((tq + 2 * BLOCK, KV_WIDTH), BF16),
                        pltpu.VMEM((tq + 2 * BLOCK, KV_WIDTH), BF16)],
        compiler_params=pltpu.CompilerParams(
            dimension_semantics=("arbitrary",), vmem_limit_bytes=_vmem_limit(blocks)),
        name="window_attn",
    )(sink, proj, proj, proj, proj, proj, proj, proj, bias)


def _t5_bucket(rel):
    nb = N_BUCKETS // 2
    max_exact = nb // 2
    ret = (rel > 0).astype(jnp.int32) * nb
    n = jnp.abs(rel)
    nf = jnp.maximum(n, 1).astype(F32)
    large = max_exact + (jnp.log(nf / max_exact) / math.log(MAX_DIST / max_exact) * (nb - max_exact)).astype(jnp.int32)
    large = jnp.minimum(large, nb - 1)
    return ret + jnp.where(n < max_exact, n, large)


def _window_bias_table(rel_bias):
    rel = jnp.arange(3 * BLOCK)[None, :] - BLOCK - jnp.arange(BLOCK)[:, None]
    bias = rel_bias.astype(F32)[_t5_bucket(rel)]
    bias = jnp.where((jnp.abs(rel) <= WINDOW)[:, :, None], bias, NEG)
    return bias.transpose(2, 0, 1)


def _spatial_gate_kernel(u_ref, v_ref, lng_ref, lnb_ref, ws_ref, bs_ref, o_ref, vn_ref, *, tm):
    o_ref[...] = _gelu(u_ref[...].astype(F32))
    gv = _gelu(v_ref[...].astype(F32))
    mu = jnp.mean(gv, axis=-1, keepdims=True)
    d = gv - mu
    var = jnp.mean(d * d, axis=-1, keepdims=True)
    vn_ref[...] = (d * lax.rsqrt(var + EPS) * lng_ref[...] + lnb_ref[...]).astype(BF16)
    low_half = lax.broadcasted_iota(jnp.int32, (CHUNK, LANES), 1) < HEAD_DIM
    for c in range(tm // CHUNK):
        rows = slice(c * CHUNK, (c + 1) * CHUNK)
        for gp in range(B_GROUPS // 2):
            cols = slice(gp * LANES, (gp + 1) * LANES)
            vp = vn_ref[rows, cols]
            r0 = jnp.dot(ws_ref[2 * gp], vp, preferred_element_type=F32)
            r1 = jnp.dot(ws_ref[2 * gp + 1], vp, preferred_element_type=F32)
            s = jnp.where(low_half, r0, r1) + bs_ref[:, cols]
            o_ref[rows, cols] = o_ref[rows, cols] * s


def _spatial_gate(proj, ln_g, ln_b, w_s, bs_tab, *, tm=512):
    m = proj.shape[0]
    blocks = 2 * (2 * tm * B_WIDTH * 2 + tm * B_WIDTH * 4) + tm * B_WIDTH * 2 + 4 * tm * B_WIDTH * 4
    return pl.pallas_call(
        functools.partial(_spatial_gate_kernel, tm=tm),
        grid=(m // tm,),
        in_specs=[
            pl.BlockSpec((tm, B_WIDTH), lambda i: (i, UB_OFF // B_WIDTH)),
            pl.BlockSpec((tm, B_WIDTH), lambda i: (i, VB_OFF // B_WIDTH)),
            pl.BlockSpec((1, B_WIDTH), lambda i: (0, 0)),
            pl.BlockSpec((1, B_WIDTH), lambda i: (0, 0)),
            pl.BlockSpec((B_GROUPS, CHUNK, CHUNK), lambda i: (0, 0, 0)),
            pl.BlockSpec((CHUNK, B_WIDTH), lambda i: (0, 0)),
        ],
        out_specs=pl.BlockSpec((tm, B_WIDTH), lambda i: (i, 0)),
        out_shape=jax.ShapeDtypeStruct((m, B_WIDTH), F32),
        scratch_shapes=[pltpu.VMEM((tm, B_WIDTH), BF16)],
        compiler_params=pltpu.CompilerParams(
            dimension_semantics=("arbitrary",), vmem_limit_bytes=_vmem_limit(blocks)),
        name="spatial_gate",
    )(proj, proj, ln_g, ln_b, w_s, bs_tab)


def _nbr_attn_kernel(q_ref, k_ref, v_ref, bias_ref, o_ref, *, rows_per_step, rows):
    rb = pl.program_id(2)
    key_len = NA_ROWS * GRID_W
    low_half = lax.broadcasted_iota(jnp.int32, (GRID_W, LANES), 1) < HEAD_DIM

    def body(j, carry):
        r = rb * rows_per_step + j
        row_start = jnp.clip(r - NA_ROWS // 2, 0, rows - NA_ROWS)
        delta = r - row_start
        k0 = pl.multiple_of(row_start * GRID_W, GRID_W)
        q0 = pl.multiple_of(j * GRID_W, GRID_W)
        kw = k_ref[pl.ds(k0, key_len), :]
        vw = v_ref[pl.ds(k0, key_len), :]
        q = q_ref[pl.ds(q0, GRID_W), :] * SCALE
        zero = jnp.zeros_like(q)
        o_pair = None
        for hh in range(2):
            qm = jnp.where(low_half, q, zero) if hh == 0 else jnp.where(low_half, zero, q)
            s = lax.dot_general(qm, kw, (((1,), (1,)), ((), ())), preferred_element_type=F32)
            s = s + bias_ref[delta, hh]
            m = jnp.max(s, axis=-1, keepdims=True)
            p = jnp.exp(s - m)
            denom = jnp.sum(p, axis=-1, keepdims=True)
            o = jnp.dot(p.astype(BF16), vw, preferred_element_type=F32) * (1.0 / denom)
            o_pair = o if hh == 0 else jnp.where(low_half, o_pair, o)
        o_ref[pl.ds(q0, GRID_W), :] = o_pair
        return carry

    lax.fori_loop(0, rows_per_step, body, 0)


def _nbr_attn(proj, bias, batch, seq_len, *, rows_per_step=8):
    m = proj.shape[0]
    tq = rows_per_step * GRID_W
    steps = seq_len // tq
    rows = seq_len // GRID_W
    assert rows >= NA_ROWS
    key_len = NA_ROWS * GRID_W
    blocks = 2 * (tq * LANES * 2 + 2 * seq_len * LANES * 2 + NA_ROWS * 2 * GRID_W * key_len * 4
                  + tq * LANES * 4)
    return pl.pallas_call(
        functools.partial(_nbr_attn_kernel, rows_per_step=rows_per_step, rows=rows),
        grid=(batch, C_HEADS // 2, steps),
        in_specs=[
            pl.BlockSpec((tq, LANES), lambda b, p, r: (b * steps + r, QC_OFF // LANES + p)),
            pl.BlockSpec((seq_len, LANES), lambda b, p, r: (b, KC_OFF // LANES + p)),
            pl.BlockSpec((seq_len, LANES), lambda b, p, r: (b, VC_OFF // LANES + p)),
            pl.BlockSpec((NA_ROWS, 2, GRID_W, key_len), lambda b, p, r: (0, p, 0, 0)),
        ],
        out_specs=pl.BlockSpec((tq, LANES), lambda b, p, r: (b * steps + r, p)),
        out_shape=jax.ShapeDtypeStruct((m, C_WIDTH), F32),
        compiler_params=pltpu.CompilerParams(
            dimension_semantics=("arbitrary", "arbitrary", "arbitrary"),
            vmem_limit_bytes=_vmem_limit(blocks)),
        name="nbr_attn",
    )(proj, proj, proj, bias)


def _nbr_bias_table(rpb):
    delta = np.arange(NA_ROWS)
    a = np.arange(NA_ROWS)
    ri = a[None, :] - delta[:, None] + NA_ROWS - 1
    qc = np.arange(GRID_W)
    kc = np.arange(GRID_W)
    col_start = np.clip(qc - NA_COLS // 2, 0, GRID_W - NA_COLS)
    valid = (kc[None, :] >= col_start[:, None]) & (kc[None, :] < col_start[:, None] + NA_COLS)
    ci = np.clip(kc[None, :] - qc[:, None] + NA_COLS - 1, 0, 2 * NA_COLS - 2)
    tab = rpb.astype(F32)[:, ri[:, None, :, None], ci[None, :, None, :]]
    tab = jnp.where(valid[None, None, :, None, :], tab, NEG)
    return tab.transpose(1, 0, 2, 3, 4).reshape(NA_ROWS, C_HEADS, GRID_W, NA_ROWS * GRID_W)


def _merge_kernel(ya_ref, yb_ref, yc_ref, h_ref, ga_ref, gb_ref, gc_ref, wa_ref, wb_ref, wc_ref, o_ref):
    acc = h_ref[...]
    acc = acc + jnp.dot(_rms(ya_ref[...], ga_ref[...]).astype(BF16), wa_ref[...], preferred_element_type=F32)
    acc = acc + jnp.dot(_rms(yb_ref[...], gb_ref[...]).astype(BF16), wb_ref[...], preferred_element_type=F32)
    acc = acc + jnp.dot(_rms(yc_ref[...], gc_ref[...]).astype(BF16), wc_ref[...], preferred_element_type=F32)
    o_ref[...] = acc


def _merge(ya, yb, yc, h, ga, gb, gc, wa, wb, wc, *, tm=256):
    m = h.shape[0]
    row = lambda w: pl.BlockSpec((tm, w), lambda i: (i, 0))
    const = lambda r, c: pl.BlockSpec((r, c), lambda i: (0, 0))
    blocks = 2 * tm * (A_WIDTH + B_WIDTH + C_WIDTH + 2 * D_MODEL) * 4 + 2 * D_MODEL * D_MODEL * 2
    return pl.pallas_call(
        _merge_kernel,
        grid=(m // tm,),
        in_specs=[row(A_WIDTH), row(B_WIDTH), row(C_WIDTH), row(D_MODEL),
                  const(1, A_WIDTH), const(1, B_WIDTH), const(1, C_WIDTH),
                  const(A_WIDTH, D_MODEL), const(B_WIDTH, D_MODEL), const(C_WIDTH, D_MODEL)],
        out_specs=row(D_MODEL),
        out_shape=jax.ShapeDtypeStruct((m, D_MODEL), F32),
        compiler_params=pltpu.CompilerParams(
            dimension_semantics=("arbitrary",), vmem_limit_bytes=_vmem_limit(blocks)),
        name="merge",
    )(ya, yb, yc, h, ga, gb, gc, wa, wb, wc)


def _conv_ffn_kernel(x_ref, xp_ref, xn_ref, g_ref, wup_ref, cw_ref, wdn_ref, fg_ref, o_ref, xs_ref, hs_ref,
                     *, tm, tiles_per_seq, final_norm):
    i = pl.program_id(0)
    j = pl.program_id(1)

    @pl.when(j == 0)
    def _():
        g = g_ref[...]
        tseq = lax.rem(i, tiles_per_seq)
        keep_prev = jnp.where(tseq == 0, 0.0, 1.0).astype(F32)
        keep_next = jnp.where(tseq == tiles_per_seq - 1, 0.0, 1.0).astype(F32)
        xs_ref[0:HALO] = (_rms(xp_ref[...], g) * keep_prev).astype(BF16)
        xs_ref[HALO:HALO + tm] = _rms(x_ref[...], g).astype(BF16)
        xs_ref[HALO + tm:] = (_rms(xn_ref[...], g) * keep_next).astype(BF16)
        o_ref[...] = x_ref[...]

    hs_ref[...] = jnp.dot(xs_ref[...], wup_ref[...], preferred_element_type=F32)
    cw = cw_ref[...]
    hc = (hs_ref[HALO - 1:HALO - 1 + tm] * cw[0:1] + hs_ref[HALO:HALO + tm] * cw[1:2]
          + hs_ref[HALO + 1:HALO + 1 + tm] * cw[2:3] + cw[3:4])
    act = (_gelu(hc[:, :FF_TILE]) * hc[:, FF_TILE:]).astype(BF16)
    o_ref[...] += jnp.dot(act, wdn_ref[...], preferred_element_type=F32)

    if final_norm:
        @pl.when(j == N_FF_TILES - 1)
        def _():
            o_ref[...] = _rms(o_ref[...], fg_ref[...])


def _conv_ffn(h, g, w_up, cw, w_down, final_g, seq_len, *, final_norm, tm=512):
    m = h.shape[0]
    halo_blocks = tm // HALO
    n_halo = m // HALO
    blocks = (2 * tm * D_MODEL * 4 * 2 + 2 * (D_MODEL * 2 * FF_TILE * 2 + FF_TILE * D_MODEL * 2)
              + (tm + 2 * HALO) * (D_MODEL * 2 + 2 * FF_TILE * 4))
    return pl.pallas_call(
        functools.partial(_conv_ffn_kernel, tm=tm, tiles_per_seq=seq_len // tm, final_norm=final_norm),
        grid=(m // tm, N_FF_TILES),
        in_specs=[
            pl.BlockSpec((tm, D_MODEL), lambda i, j: (i, 0)),
            pl.BlockSpec((HALO, D_MODEL), lambda i, j: (jnp.maximum(i * halo_blocks - 1, 0), 0)),
            pl.BlockSpec((HALO, D_MODEL), lambda i, j: (jnp.minimum((i + 1) * halo_blocks, n_halo - 1), 0)),
            pl.BlockSpec((1, D_MODEL), lambda i, j: (0, 0)),
            pl.BlockSpec((D_MODEL, 2 * FF_TILE), lambda i, j: (0, j)),
            pl.BlockSpec((8, 2 * FF_TILE), lambda i, j: (0, j)),
            pl.BlockSpec((FF_TILE, D_MODEL), lambda i, j: (j, 0)),
            pl.BlockSpec((1, D_MODEL), lambda i, j: (0, 0)),
        ],
        out_specs=pl.BlockSpec((tm, D_MODEL), lambda i, j: (i, 0)),
        out_shape=jax.ShapeDtypeStruct((m, D_MODEL), F32),
        scratch_shapes=[pltpu.VMEM((tm + 2 * HALO, D_MODEL), BF16),
                        pltpu.VMEM((tm + 2 * HALO, 2 * FF_TILE), F32)],
        compiler_params=pltpu.CompilerParams(
            dimension_semantics=("arbitrary", "arbitrary"), vmem_limit_bytes=_vmem_limit(blocks)),
        name="conv_ffn",
    )(h, h, h, g, w_up, cw, w_down, final_g)


def _interleave_ff(a):
    lead = a.shape[:-1]
    halves = a.reshape(lead + (2, D_FF))
    halves = jnp.pad(halves, [(0, 0)] * len(lead) + [(0, 0), (0, D_FF_PAD - D_FF)])
    tiles = halves.reshape(lead + (2, N_FF_TILES, FF_TILE))
    tiles = jnp.swapaxes(tiles, -3, -2)
    return tiles.reshape(lead + (N_FF_TILES * 2 * FF_TILE,))


def _prepare_params(rel_bias, norm1_g, w_in, sink, sgu_ln_g, sgu_ln_b, w_spatial, b_spatial, na_rpb,
                    gn_a, gn_b, gn_c, w_out, norm2_g, w_up, conv_w, conv_b, w_down, final_g):
    src = [int(o) for o in np.cumsum((A_WIDTH, KV_WIDTH, KV_WIDTH, B_WIDTH, B_WIDTH, C_WIDTH, C_WIDTH))]
    qa, ka, va, ub, vb, qc, kc, vc = jnp.split(w_in.astype(BF16), src, axis=-1)
    w_in_cols = jnp.concatenate([qa, qc, kc, vc, ub, vb, ka, va], axis=-1)
    cw = jnp.concatenate([_interleave_ff(conv_w), _interleave_ff(conv_b)[:, None, :],
                          jnp.zeros((DEPTH, 4, 2 * D_FF_PAD), F32)], axis=1)
    w_out_b = w_out.astype(BF16)
    return dict(
        norm1_g=norm1_g[:, None, :],
        w_in=w_in_cols,
        bias_a=_window_bias_table(rel_bias),
        sink=sink.astype(F32),
        ln_g=sgu_ln_g[:, None, :], ln_b=sgu_ln_b[:, None, :],
        w_s=w_spatial.astype(BF16),
        bs_tab=jnp.repeat(jnp.swapaxes(b_spatial, 1, 2), HEAD_DIM, axis=2).astype(F32),
        bias_c=jax.vmap(_nbr_bias_table)(na_rpb),
        gn_a=gn_a[:, None, :], gn_b=gn_b[:, None, :], gn_c=gn_c[:, None, :],
        wo_a=w_out_b[:, :A_WIDTH], wo_b=w_out_b[:, A_WIDTH:A_WIDTH + B_WIDTH], wo_c=w_out_b[:, A_WIDTH + B_WIDTH:],
        norm2_g=norm2_g[:, None, :],
        w_up=_interleave_ff(w_up).astype(BF16),
        cw=cw,
        w_down=jnp.pad(w_down, ((0, 0), (0, D_FF_PAD - D_FF), (0, 0))).astype(BF16),
        final_g=final_g[None, :],
    )


def _trunk(x, p):
    batch, seq_len, _ = x.shape
    h = x.reshape(batch * seq_len, D_MODEL)
    for l in range(DEPTH):
        proj = _norm_proj(h, p["norm1_g"][l], p["w_in"][l])
        ya = _window_attn(proj, p["bias_a"], p["sink"][l], seq_len)
        yb = _spatial_gate(proj, p["ln_g"][l], p["ln_b"][l], p["w_s"][l], p["bs_tab"][l])
        yc = _nbr_attn(proj, p["bias_c"][l], batch, seq_len)
        h = _merge(ya, yb, yc, h, p["gn_a"][l], p["gn_b"][l], p["gn_c"][l],
                   p["wo_a"][l], p["wo_b"][l], p["wo_c"][l])
        h = _conv_ffn(h, p["norm2_g"][l], p["w_up"][l], p["cw"][l], p["w_down"][l], p["final_g"], seq_len,
                      final_norm=(l == DEPTH - 1))
    return h.reshape(batch, seq_len, D_MODEL)


def kernel(x_prompt, x_sample, rel_bias, norm1_g, w_in, sink, sgu_ln_g, sgu_ln_b, w_spatial, b_spatial, na_rpb, gn_a, gn_b, gn_c, w_out, norm2_g, w_up, conv_w, conv_b, w_down, final_g):
    p = _prepare_params(rel_bias, norm1_g, w_in, sink, sgu_ln_g, sgu_ln_b, w_spatial, b_spatial, na_rpb,
                        gn_a, gn_b, gn_c, w_out, norm2_g, w_up, conv_w, conv_b, w_down, final_g)
    return (_trunk(x_prompt, p), _trunk(x_sample, p))
```

```python
import functools
import math

import jax
import jax.numpy as jnp
import numpy as np
from jax import lax
from jax.experimental import pallas as pl
from jax.experimental.pallas import tpu as pltpu

F32 = jnp.float32
BF16 = jnp.bfloat16

D_MODEL = 2048
DEPTH = 4
HEAD_DIM = 64
A_HEADS = 12
A_KV_HEADS = 4
A_GROUP = A_HEADS // A_KV_HEADS
WINDOW = 128
BLOCK = 128
N_BUCKETS = 32
MAX_DIST = 128
B_GROUPS = 8
CHUNK = 128
C_HEADS = 12
GRID_W = 64
NA_ROWS = 8
NA_COLS = 16
A_WIDTH = A_HEADS * HEAD_DIM
B_WIDTH = B_GROUPS * HEAD_DIM
C_WIDTH = C_HEADS * HEAD_DIM
KV_WIDTH = A_KV_HEADS * HEAD_DIM
IN_WIDTH = A_WIDTH + 2 * KV_WIDTH + 2 * B_WIDTH + 3 * C_WIDTH
D_FF = 5504
EPS = 1e-6
NEG = -1e30
SCALE = HEAD_DIM ** -0.5

LANES = 128
V7X_VMEM_BYTES = 64 * 1024 * 1024
V7X_VMEM_CAP = 56 * 1024 * 1024

QA_OFF = 0
QC_OFF = QA_OFF + A_WIDTH
KC_OFF = QC_OFF + C_WIDTH
VC_OFF = KC_OFF + C_WIDTH
UB_OFF = VC_OFF + C_WIDTH
VB_OFF = UB_OFF + B_WIDTH
KA_OFF = VB_OFF + B_WIDTH
VA_OFF = KA_OFF + KV_WIDTH

A_KV_TILES = A_KV_HEADS // 2
A_HEAD_ORDER = [kv * A_GROUP + g for u in range(A_KV_TILES) for g in range(A_GROUP) for kv in (2 * u, 2 * u + 1)]

FF_TILE = 512
D_FF_PAD = -(-D_FF // FF_TILE) * FF_TILE
N_FF_TILES = D_FF_PAD // FF_TILE
HALO = 8


def _vmem_limit(block_bytes):
    return int(min(V7X_VMEM_CAP, max(32 * 1024 * 1024, 2 * block_bytes)))


def _gelu(x):
    c = math.sqrt(2.0 / math.pi)
    return 0.5 * x * (1.0 + jnp.tanh(c * (x + 0.044715 * (x * x * x))))


def _rms(x, g):
    ms = jnp.mean(x * x, axis=-1, keepdims=True)
    return x * lax.rsqrt(ms + EPS) * g


def _nt_dot(a, b):
    return lax.dot_general(a, b, (((1,), (1,)), ((), ())), preferred_element_type=F32)


def _split_heads_lhs(q, low_half):
    zero = jnp.zeros_like(q)
    return jnp.concatenate([jnp.where(low_half, q, zero), jnp.where(low_half, zero, q)], axis=0)


def _norm_proj_kernel(x_ref, g_ref, w_ref, o_ref, xn_ref, *, n_chunk):
    xn_ref[...] = _rms(x_ref[...], g_ref[...]).astype(BF16)
    for c in range(0, IN_WIDTH, n_chunk):
        o_ref[:, c:c + n_chunk] = jnp.dot(
            xn_ref[...], w_ref[:, c:c + n_chunk], preferred_element_type=F32).astype(BF16)


def _norm_proj(h, g, w, layer, *, tm=512, n_chunk=512):
    m = h.shape[0]
    blocks = 2 * tm * D_MODEL * 4 + D_MODEL * IN_WIDTH * 2 + 2 * tm * IN_WIDTH * 2 + tm * D_MODEL * 2
    return pl.pallas_call(
        functools.partial(_norm_proj_kernel, n_chunk=n_chunk),
        grid=(m // tm,),
        in_specs=[
            pl.BlockSpec((tm, D_MODEL), lambda i: (i, 0)),
            pl.BlockSpec((None, 1, D_MODEL), lambda i: (layer, 0, 0)),
            pl.BlockSpec((None, D_MODEL, IN_WIDTH), lambda i: (layer, 0, 0), pipeline_mode=pl.Buffered(1)),
        ],
        out_specs=pl.BlockSpec((tm, IN_WIDTH), lambda i: (i, 0)),
        out_shape=jax.ShapeDtypeStruct((m, IN_WIDTH), BF16),
        scratch_shapes=[pltpu.VMEM((tm, D_MODEL), BF16)],
        compiler_params=pltpu.CompilerParams(
            dimension_semantics=("arbitrary",), vmem_limit_bytes=_vmem_limit(blocks)),
        name="norm_proj",
    )(h, g, w)


def _window_attn_kernel(sink_ref, q_ref, k_ref, v_ref, kp_ref, kn_ref, vp_ref, vn_ref, bias_ref, o_ref,
                        kbuf, vbuf, *, tq, blocks_per_seq, layer):
    i = pl.program_id(0)
    nb = tq // BLOCK
    band = 3 * BLOCK
    kbuf[0:BLOCK] = kp_ref[...]
    kbuf[BLOCK:BLOCK + tq] = k_ref[...]
    kbuf[BLOCK + tq:] = kn_ref[...]
    vbuf[0:BLOCK] = vp_ref[...]
    vbuf[BLOCK:BLOCK + tq] = v_ref[...]
    vbuf[BLOCK + tq:] = vn_ref[...]
    col = lax.broadcasted_iota(jnp.int32, (1, band), 1)
    low_half = lax.broadcasted_iota(jnp.int32, (A_GROUP * BLOCK, LANES), 1) < HEAD_DIM
    low_half_blk = lax.broadcasted_iota(jnp.int32, (BLOCK, LANES), 1) < HEAD_DIM

    def body(j, carry):
        nseq = lax.rem(i * nb + j, blocks_per_seq)
        lo = jnp.where(nseq == 0, NEG, 0.0).astype(F32)
        hi = jnp.where(nseq == blocks_per_seq - 1, NEG, 0.0).astype(F32)
        pen = jnp.where(col < BLOCK, lo, jnp.where(col >= 2 * BLOCK, hi, 0.0))
        r0 = pl.multiple_of(j * BLOCK, BLOCK)
        for u in range(A_KV_TILES):
            lanes = slice(u * LANES, (u + 1) * LANES)
            kt = kbuf[pl.ds(r0, band), lanes]
            vt = vbuf[pl.ds(r0, band), lanes]
            q = jnp.concatenate(
                [q_ref[pl.ds(r0, BLOCK), (A_GROUP * u + g) * LANES:(A_GROUP * u + g + 1) * LANES]
                 for g in range(A_GROUP)], axis=0) * SCALE
            s = _nt_dot(_split_heads_lhs(q, low_half), kt) + bias_ref[u] + pen
            probs, inv = [], []
            for n in range(2 * A_GROUP):
                sn = s[n * BLOCK:(n + 1) * BLOCK]
                sink = sink_ref[layer, 2 * A_GROUP * u + n]
                m = jnp.maximum(jnp.max(sn, axis=-1, keepdims=True), sink)
                p = jnp.exp(sn - m)
                inv.append(1.0 / (jnp.sum(p, axis=-1, keepdims=True) + jnp.exp(sink - m)))
                probs.append(p.astype(BF16))
            o = jnp.dot(jnp.concatenate(probs, axis=0), vt, preferred_element_type=F32)
            for g in range(A_GROUP):
                o_lo = o[g * BLOCK:(g + 1) * BLOCK] * inv[g]
                o_hi = o[(A_GROUP + g) * BLOCK:(A_GROUP + g + 1) * BLOCK] * inv[A_GROUP + g]
                o_ref[pl.ds(r0, BLOCK), (A_GROUP * u + g) * LANES:(A_GROUP * u + g + 1) * LANES] = (
                    jnp.where(low_half_blk, o_lo, o_hi))
        return carry

    lax.fori_loop(0, nb, body, 0)


def _window_attn(proj, bias, sink, layer, seq_len, *, tq=512):
    m = proj.shape[0]
    nb = tq // BLOCK
    n_blocks = m // BLOCK
    ka, va = KA_OFF // KV_WIDTH, VA_OFF // KV_WIDTH
    blocks = 2 * (tq * A_WIDTH * 2 + 2 * (tq + 2 * BLOCK) * KV_WIDTH * 2 + tq * A_WIDTH * 4
                  + A_HEADS * BLOCK * 3 * BLOCK * 4) + 2 * (tq + 2 * BLOCK) * KV_WIDTH * 2
    prev_map = lambda c: (lambda i: (jnp.maximum(i * nb - 1, 0), c))
    next_map = lambda c: (lambda i: (jnp.minimum((i + 1) * nb, n_blocks - 1), c))
    return pl.pallas_call(
        functools.partial(_window_attn_kernel, tq=tq, blocks_per_seq=seq_len // BLOCK, layer=layer),
        grid=(m // tq,),
        in_specs=[
            pl.BlockSpec(memory_space=pltpu.SMEM),
            pl.BlockSpec((tq, A_WIDTH), lambda i: (i, QA_OFF // A_WIDTH)),
            pl.BlockSpec((tq, KV_WIDTH), lambda i: (i, ka)),
            pl.BlockSpec((tq, KV_WIDTH), lambda i: (i, va)),
            pl.BlockSpec((BLOCK, KV_WIDTH), prev_map(ka)),
            pl.BlockSpec((BLOCK, KV_WIDTH), next_map(ka)),
            pl.BlockSpec((BLOCK, KV_WIDTH), prev_map(va)),
            pl.BlockSpec((BLOCK, KV_WIDTH), next_map(va)),
            pl.BlockSpec((A_KV_TILES, 2 * A_GROUP * BLOCK, 3 * BLOCK), lambda i: (0, 0, 0)),
        ],
        out_specs=pl.BlockSpec((tq, A_WIDTH), lambda i: (i, 0)),
        out_shape=jax.ShapeDtypeStruct((m, A_WIDTH), F32),
        scratch_shapes=[pltpu.VMEM((tq + 2 * BLOCK, KV_WIDTH), BF16),
                        pltpu.VMEM((tq + 2 * BLOCK, KV_WIDTH), BF16)],
        compiler_params=pltpu.CompilerParams(
            dimension_semantics=("arbitrary",), vmem_limit_bytes=_vmem_limit(blocks)),
        name="window_attn",
    )(sink, proj, proj, proj, proj, proj, proj, proj, bias)


def _t5_bucket(rel):
    nb = N_BUCKETS // 2
    max_exact = nb // 2
    ret = (rel > 0).astype(jnp.int32) * nb
    n = jnp.abs(rel)
    nf = jnp.maximum(n, 1).astype(F32)
    large = max_exact + (jnp.log(nf / max_exact) / math.log(MAX_DIST / max_exact) * (nb - max_exact)).astype(jnp.int32)
    large = jnp.minimum(large, nb - 1)
    return ret + jnp.where(n < max_exact, n, large)


def _window_bias_table(rel_bias):
    rel = jnp.arange(3 * BLOCK)[None, :] - BLOCK - jnp.arange(BLOCK)[:, None]
    bias = rel_bias.astype(F32)[_t5_bucket(rel)]
    bias = jnp.where((jnp.abs(rel) <= WINDOW)[:, :, None], bias, NEG)
    return bias.transpose(2, 0, 1).reshape(A_KV_TILES, 2 * A_GROUP * BLOCK, 3 * BLOCK)


def _spatial_gate_kernel(u_ref, v_ref, lng_ref, lnb_ref, ws_ref, bs_ref, o_ref, vn_ref, *, tm):
    o_ref[...] = _gelu(u_ref[...].astype(F32))
    gv = _gelu(v_ref[...].astype(F32))
    mu = jnp.mean(gv, axis=-1, keepdims=True)
    d = gv - mu
    var = jnp.mean(d * d, axis=-1, keepdims=True)
    vn_ref[...] = (d * lax.rsqrt(var + EPS) * lng_ref[...] + lnb_ref[...]).astype(BF16)
    low_half = lax.broadcasted_iota(jnp.int32, (CHUNK, LANES), 1) < HEAD_DIM
    for c in range(tm // CHUNK):
        rows = slice(c * CHUNK, (c + 1) * CHUNK)
        for gp in range(B_GROUPS // 2):
            cols = slice(gp * LANES, (gp + 1) * LANES)
            vp = vn_ref[rows, cols]
            r0 = jnp.dot(ws_ref[2 * gp], vp, preferred_element_type=F32)
            r1 = jnp.dot(ws_ref[2 * gp + 1], vp, preferred_element_type=F32)
            s = jnp.where(low_half, r0, r1) + bs_ref[:, cols]
            o_ref[rows, cols] = o_ref[rows, cols] * s


def _spatial_gate(proj, ln_g, ln_b, w_s, bs_tab, layer, *, tm=512):
    m = proj.shape[0]
    blocks = 2 * (2 * tm * B_WIDTH * 2 + tm * B_WIDTH * 4) + tm * B_WIDTH * 2 + 4 * tm * B_WIDTH * 4
    return pl.pallas_call(
        functools.partial(_spatial_gate_kernel, tm=tm),
        grid=(m // tm,),
        in_specs=[
            pl.BlockSpec((tm, B_WIDTH), lambda i: (i, UB_OFF // B_WIDTH)),
            pl.BlockSpec((tm, B_WIDTH), lambda i: (i, VB_OFF // B_WIDTH)),
            pl.BlockSpec((None, 1, B_WIDTH), lambda i: (layer, 0, 0)),
            pl.BlockSpec((None, 1, B_WIDTH), lambda i: (layer, 0, 0)),
            pl.BlockSpec((None, B_GROUPS, CHUNK, CHUNK), lambda i: (layer, 0, 0, 0)),
            pl.BlockSpec((None, CHUNK, B_WIDTH), lambda i: (layer, 0, 0)),
        ],
        out_specs=pl.BlockSpec((tm, B_WIDTH), lambda i: (i, 0)),
        out_shape=jax.ShapeDtypeStruct((m, B_WIDTH), F32),
        scratch_shapes=[pltpu.VMEM((tm, B_WIDTH), BF16)],
        compiler_params=pltpu.CompilerParams(
            dimension_semantics=("arbitrary",), vmem_limit_bytes=_vmem_limit(blocks)),
        name="spatial_gate",
    )(proj, proj, ln_g, ln_b, w_s, bs_tab)


NA_KEYS = NA_ROWS * GRID_W
NA_BIAS_TILES = 2 * NA_ROWS - 2


def _nbr_attn_kernel(q_ref, k_ref, v_ref, bias_ref, o_ref, *, rows_per_step, rows):
    rb = pl.program_id(2)
    low_half = lax.broadcasted_iota(jnp.int32, (GRID_W, LANES), 1) < HEAD_DIM
    for j in range(rows_per_step):
        r = rb * rows_per_step + j
        row_start = jnp.clip(r - NA_ROWS // 2, 0, rows - NA_ROWS)
        delta = r - row_start
        k0 = pl.multiple_of(row_start * GRID_W, GRID_W)
        q_rows = slice(j * GRID_W, (j + 1) * GRID_W)
        kw = k_ref[pl.ds(k0, NA_KEYS), :]
        vw = v_ref[pl.ds(k0, NA_KEYS), :]
        s = _nt_dot(_split_heads_lhs(q_ref[q_rows, :] * SCALE, low_half), kw)
        s = s + jnp.concatenate(
            [bias_ref[2 * t - delta + NA_ROWS - 1] for t in range(NA_ROWS // 2)], axis=1)
        m = jnp.max(s, axis=-1, keepdims=True)
        p = jnp.exp(s - m)
        inv = 1.0 / jnp.sum(p, axis=-1, keepdims=True)
        o = jnp.dot(p.astype(BF16), vw, preferred_element_type=F32) * inv
        o_ref[q_rows, :] = jnp.where(low_half, o[:GRID_W], o[GRID_W:])


def _nbr_attn(proj, bias, layer, batch, seq_len, *, rows_per_step=8):
    m = proj.shape[0]
    tq = rows_per_step * GRID_W
    steps = seq_len // tq
    rows = seq_len // GRID_W
    assert rows >= NA_ROWS
    blocks = 2 * (tq * LANES * 2 + 2 * seq_len * LANES * 2 + NA_BIAS_TILES * 2 * GRID_W * LANES * 4
                  + tq * LANES * 4)
    return pl.pallas_call(
        functools.partial(_nbr_attn_kernel, rows_per_step=rows_per_step, rows=rows),
        grid=(batch, C_HEADS // 2, steps),
        in_specs=[
            pl.BlockSpec((tq, LANES), lambda b, p, r: (b * steps + r, QC_OFF // LANES + p)),
            pl.BlockSpec((seq_len, LANES), lambda b, p, r: (b, KC_OFF // LANES + p)),
            pl.BlockSpec((seq_len, LANES), lambda b, p, r: (b, VC_OFF // LANES + p)),
            pl.BlockSpec((None, None, NA_BIAS_TILES, 2 * GRID_W, LANES), lambda b, p, r: (layer, p, 0, 0, 0)),
        ],
        out_specs=pl.BlockSpec((tq, LANES), lambda b, p, r: (b * steps + r, p)),
        out_shape=jax.ShapeDtypeStruct((m, C_WIDTH), F32),
        compiler_params=pltpu.CompilerParams(
            dimension_semantics=("arbitrary", "arbitrary", "arbitrary"),
            vmem_limit_bytes=_vmem_limit(blocks)),
        name="nbr_attn",
    )(proj, proj, proj, bias)


def _nbr_bias_table(rpb):
    qc = np.arange(GRID_W)
    kc = np.arange(GRID_W)
    col_start = np.clip(qc - NA_COLS // 2, 0, GRID_W - NA_COLS)
    valid = (kc[None, :] >= col_start[:, None]) & (kc[None, :] < col_start[:, None] + NA_COLS)
    ci = np.clip(kc[None, :] - qc[:, None] + NA_COLS - 1, 0, 2 * NA_COLS - 2)
    per_row = jnp.take(rpb.astype(F32), jnp.asarray(ci), axis=2)
    per_row = jnp.where(valid[None, None], per_row, NEG)
    two_rows = jnp.concatenate([per_row[:, :-1], per_row[:, 1:]], axis=-1)
    two_rows = two_rows.reshape(C_HEADS // 2, 2, NA_BIAS_TILES, GRID_W, 2 * GRID_W)
    return two_rows.transpose(0, 2, 1, 3, 4).reshape(C_HEADS // 2, NA_BIAS_TILES, 2 * GRID_W, 2 * GRID_W)


def _merge_kernel(ya_ref, yb_ref, yc_ref, h_ref, ga_ref, gb_ref, gc_ref, wa_ref, wb_ref, wc_ref, o_ref):
    acc = h_ref[...]
    acc = acc + jnp.dot(_rms(ya_ref[...], ga_ref[...]).astype(BF16), wa_ref[...], preferred_element_type=F32)
    acc = acc + jnp.dot(_rms(yb_ref[...], gb_ref[...]).astype(BF16), wb_ref[...], preferred_element_type=F32)
    acc = acc + jnp.dot(_rms(yc_ref[...], gc_ref[...]).astype(BF16), wc_ref[...], preferred_element_type=F32)
    o_ref[...] = acc


def _merge(ya, yb, yc, h, ga, gb, gc, wa, wb, wc, layer, *, tm=256):
    m = h.shape[0]
    row = lambda w: pl.BlockSpec((tm, w), lambda i: (i, 0))
    const = lambda r, c: pl.BlockSpec((None, r, c), lambda i: (layer, 0, 0))
    blocks = 2 * tm * (A_WIDTH + B_WIDTH + C_WIDTH + 2 * D_MODEL) * 4 + 2 * D_MODEL * D_MODEL * 2
    return pl.pallas_call(
        _merge_kernel,
        grid=(m // tm,),
        in_specs=[row(A_WIDTH), row(B_WIDTH), row(C_WIDTH), row(D_MODEL),
                  const(1, A_WIDTH), const(1, B_WIDTH), const(1, C_WIDTH),
                  const(A_WIDTH, D_MODEL), const(B_WIDTH, D_MODEL), const(C_WIDTH, D_MODEL)],
        out_specs=row(D_MODEL),
        out_shape=jax.ShapeDtypeStruct((m, D_MODEL), F32),
        compiler_params=pltpu.CompilerParams(
            dimension_semantics=("arbitrary",), vmem_limit_bytes=_vmem_limit(blocks)),
        name="merge",
    )(ya, yb, yc, h, ga, gb, gc, wa, wb, wc)


def _conv3(hs_ref, cw, tm):
    return (hs_ref[HALO - 1:HALO - 1 + tm] * cw[0:1] + hs_ref[HALO:HALO + tm] * cw[1:2]
            + hs_ref[HALO + 1:HALO + 1 + tm] * cw[2:3] + cw[3:4])


def _conv_ffn_kernel(x_ref, xp_ref, xn_ref, g_ref, wg_ref, wv_ref, cwg_ref, cwv_ref, wdn_ref, fg_ref, o_ref,
                     xs_ref, hg_ref, hv_ref, *, tm, tiles_per_seq, final_norm):
    i = pl.program_id(0)
    j = pl.program_id(1)

    @pl.when(j == 0)
    def _():
        g = g_ref[...]
        tseq = lax.rem(i, tiles_per_seq)
        keep_prev = jnp.where(tseq == 0, 0.0, 1.0).astype(F32)
        keep_next = jnp.where(tseq == tiles_per_seq - 1, 0.0, 1.0).astype(F32)
        xs_ref[0:HALO] = (_rms(xp_ref[...], g) * keep_prev).astype(BF16)
        xs_ref[HALO:HALO + tm] = _rms(x_ref[...], g).astype(BF16)
        xs_ref[HALO + tm:] = (_rms(xn_ref[...], g) * keep_next).astype(BF16)
        o_ref[...] = x_ref[...]

    hg_ref[...] = jnp.dot(xs_ref[...], wg_ref[...], preferred_element_type=F32)
    hv_ref[...] = jnp.dot(xs_ref[...], wv_ref[...], preferred_element_type=F32)
    act = (_gelu(_conv3(hg_ref, cwg_ref[...], tm)) * _conv3(hv_ref, cwv_ref[...], tm)).astype(BF16)
    o_ref[...] += jnp.dot(act, wdn_ref[...], preferred_element_type=F32)

    if final_norm:
        @pl.when(j == N_FF_TILES - 1)
        def _():
            o_ref[...] = _rms(o_ref[...], fg_ref[...])


def _conv_ffn(h, g, w_gate, w_val, cw_gate, cw_val, w_down, final_g, layer, seq_len, *, final_norm, tm=512):
    m = h.shape[0]
    halo_blocks = tm // HALO
    n_halo = m // HALO
    blocks = (2 * tm * D_MODEL * 4 * 2 + 2 * (D_MODEL * 2 * FF_TILE * 2 + FF_TILE * D_MODEL * 2)
              + (tm + 2 * HALO) * (D_MODEL * 2 + 2 * FF_TILE * 4))
    return pl.pallas_call(
        functools.partial(_conv_ffn_kernel, tm=tm, tiles_per_seq=seq_len // tm, final_norm=final_norm),
        grid=(m // tm, N_FF_TILES),
        in_specs=[
            pl.BlockSpec((tm, D_MODEL), lambda i, j: (i, 0)),
            pl.BlockSpec((HALO, D_MODEL), lambda i, j: (jnp.maximum(i * halo_blocks - 1, 0), 0)),
            pl.BlockSpec((HALO, D_MODEL), lambda i, j: (jnp.minimum((i + 1) * halo_blocks, n_halo - 1), 0)),
            pl.BlockSpec((None, 1, D_MODEL), lambda i, j: (layer, 0, 0)),
            pl.BlockSpec((None, D_MODEL, FF_TILE), lambda i, j: (layer, 0, j)),
            pl.BlockSpec((None, D_MODEL, FF_TILE), lambda i, j: (layer, 0, j)),
            pl.BlockSpec((None, 8, FF_TILE), lambda i, j: (layer, 0, j)),
            pl.BlockSpec((None, 8, FF_TILE), lambda i, j: (layer, 0, j)),
            pl.BlockSpec((None, FF_TILE, D_MODEL), lambda i, j: (layer, j, 0)),
            pl.BlockSpec((1, D_MODEL), lambda i, j: (0, 0)),
        ],
        out_specs=pl.BlockSpec((tm, D_MODEL), lambda i, j: (i, 0)),
        out_shape=jax.ShapeDtypeStruct((m, D_MODEL), F32),
        scratch_shapes=[pltpu.VMEM((tm + 2 * HALO, D_MODEL), BF16),
                        pltpu.VMEM((tm + 2 * HALO, FF_TILE), F32),
                        pltpu.VMEM((tm + 2 * HALO, FF_TILE), F32)],
        compiler_params=pltpu.CompilerParams(
            dimension_semantics=("arbitrary", "arbitrary"), vmem_limit_bytes=_vmem_limit(blocks)),
        name="conv_ffn",
    )(h, h, h, g, w_gate, w_val, cw_gate, cw_val, w_down, final_g)


def _reorder_heads(a, axis, order):
    slabs = jnp.split(a, a.shape[axis] // HEAD_DIM, axis=axis)
    return jnp.concatenate([slabs[h] for h in order], axis=axis)


def _pad_ff(a):
    return jnp.pad(a, [(0, 0)] * (a.ndim - 1) + [(0, D_FF_PAD - D_FF)])


def _conv_rows(conv_w, conv_b):
    rows = jnp.concatenate([conv_w, conv_b[:, None, :]], axis=1)
    return jnp.pad(_pad_ff(rows), ((0, 0), (0, 8 - rows.shape[1]), (0, 0)))


def _prepare_params(rel_bias, norm1_g, w_in, sink, sgu_ln_g, sgu_ln_b, w_spatial, b_spatial, na_rpb,
                    gn_a, gn_b, gn_c, w_out, norm2_g, w_up, conv_w, conv_b, w_down, final_g):
    src = [int(o) for o in np.cumsum((A_WIDTH, KV_WIDTH, KV_WIDTH, B_WIDTH, B_WIDTH, C_WIDTH, C_WIDTH))]
    qa, ka, va, ub, vb, qc, kc, vc = jnp.split(w_in.astype(BF16), src, axis=-1)
    w_in_cols = jnp.concatenate([_reorder_heads(qa, 2, A_HEAD_ORDER), qc, kc, vc, ub, vb, ka, va], axis=-1)
    w_out_b = w_out.astype(BF16)
    return dict(
        norm1_g=norm1_g[:, None, :],
        w_in=w_in_cols,
        bias_a=_window_bias_table(rel_bias),
        sink=sink.astype(F32),
        ln_g=sgu_ln_g[:, None, :], ln_b=sgu_ln_b[:, None, :],
        w_s=w_spatial.astype(BF16),
        bs_tab=jnp.repeat(jnp.swapaxes(b_spatial, 1, 2), HEAD_DIM, axis=2).astype(F32),
        bias_c=jax.vmap(_nbr_bias_table)(na_rpb),
        gn_a=_reorder_heads(gn_a, 1, A_HEAD_ORDER)[:, None, :], gn_b=gn_b[:, None, :], gn_c=gn_c[:, None, :],
        wo_a=_reorder_heads(w_out_b[:, :A_WIDTH], 1, A_HEAD_ORDER),
        wo_b=w_out_b[:, A_WIDTH:A_WIDTH + B_WIDTH], wo_c=w_out_b[:, A_WIDTH + B_WIDTH:],
        norm2_g=norm2_g[:, None, :],
        w_gate=_pad_ff(w_up[:, :, :D_FF]).astype(BF16), w_val=_pad_ff(w_up[:, :, D_FF:]).astype(BF16),
        cw_gate=_conv_rows(conv_w[:, :, :D_FF], conv_b[:, :D_FF]),
        cw_val=_conv_rows(conv_w[:, :, D_FF:], conv_b[:, D_FF:]),
        w_down=jnp.pad(w_down, ((0, 0), (0, D_FF_PAD - D_FF), (0, 0))).astype(BF16),
        final_g=final_g[None, :],
    )


def _trunk(x, p):
    batch, seq_len, _ = x.shape
    h = x.reshape(batch * seq_len, D_MODEL)
    for l in range(DEPTH):
        proj = _norm_proj(h, p["norm1_g"], p["w_in"], l)
        ya = _window_attn(proj, p["bias_a"], p["sink"], l, seq_len)
        yb = _spatial_gate(proj, p["ln_g"], p["ln_b"], p["w_s"], p["bs_tab"], l)
        yc = _nbr_attn(proj, p["bias_c"], l, batch, seq_len)
        h = _merge(ya, yb, yc, h, p["gn_a"], p["gn_b"], p["gn_c"], p["wo_a"], p["wo_b"], p["wo_c"], l)
        h = _conv_ffn(h, p["norm2_g"], p["w_gate"], p["w_val"], p["cw_gate"], p["cw_val"], p["w_down"],
                      p["final_g"], l, seq_len, final_norm=(l == DEPTH - 1))
    return h.reshape(batch, seq_len, D_MODEL)


def kernel(x_prompt, x_sample, rel_bias, norm1_g, w_in, sink, sgu_ln_g, sgu_ln_b, w_spatial, b_spatial, na_rpb, gn_a, gn_b, gn_c, w_out, norm2_g, w_up, conv_w, conv_b, w_down, final_g):
    p = _prepare_params(rel_bias, norm1_g, w_in, sink, sgu_ln_g, sgu_ln_b, w_spatial, b_spatial, na_rpb,
                        gn_a, gn_b, gn_c, w_out, norm2_g, w_up, conv_w, conv_b, w_down, final_g)
    return (_trunk(x_prompt, p), _trunk(x_sample, p))
```

```python
import functools
import math

import jax
import jax.numpy as jnp
import numpy as np
from jax import lax
from jax.experimental import pallas as pl
from jax.experimental.pallas import tpu as pltpu

F32 = jnp.float32
BF16 = jnp.bfloat16

D_MODEL = 2048
DEPTH = 4
HEAD_DIM = 64
A_HEADS = 12
A_KV_HEADS = 4
A_GROUP = A_HEADS // A_KV_HEADS
WINDOW = 128
BLOCK = 128
N_BUCKETS = 32
MAX_DIST = 128
B_GROUPS = 8
CHUNK = 128
C_HEADS = 12
GRID_W = 64
NA_ROWS = 8
NA_COLS = 16
A_WIDTH = A_HEADS * HEAD_DIM
B_WIDTH = B_GROUPS * HEAD_DIM
C_WIDTH = C_HEADS * HEAD_DIM
KV_WIDTH = A_KV_HEADS * HEAD_DIM
IN_WIDTH = A_WIDTH + 2 * KV_WIDTH + 2 * B_WIDTH + 3 * C_WIDTH
D_FF = 5504
EPS = 1e-6
NEG = -1e30
SCALE = HEAD_DIM ** -0.5

LANES = 128
V7X_VMEM_BYTES = 64 * 1024 * 1024
V7X_VMEM_CAP = 56 * 1024 * 1024

QA_OFF = 0
QC_OFF = QA_OFF + A_WIDTH
KC_OFF = QC_OFF + C_WIDTH
VC_OFF = KC_OFF + C_WIDTH
UB_OFF = VC_OFF + C_WIDTH
VB_OFF = UB_OFF + B_WIDTH
KA_OFF = VB_OFF + B_WIDTH
VA_OFF = KA_OFF + KV_WIDTH

A_KV_TILES = A_KV_HEADS // 2
A_HEAD_ORDER = [kv * A_GROUP + g for u in range(A_KV_TILES) for g in range(A_GROUP) for kv in (2 * u, 2 * u + 1)]

FF_TILE = 512
D_FF_PAD = -(-D_FF // FF_TILE) * FF_TILE
N_FF_TILES = D_FF_PAD // FF_TILE
HALO = 8


def _vmem_limit(block_bytes):
    return int(min(V7X_VMEM_CAP, max(32 * 1024 * 1024, 2 * block_bytes)))


def _gelu(x):
    c = math.sqrt(2.0 / math.pi)
    return 0.5 * x * (1.0 + jnp.tanh(c * (x + 0.044715 * (x * x * x))))


def _rms(x, g):
    ms = jnp.mean(x * x, axis=-1, keepdims=True)
    return x * lax.rsqrt(ms + EPS) * g


def _nt_dot(a, b):
    return lax.dot_general(a, b, (((1,), (1,)), ((), ())), preferred_element_type=F32)


def _split_heads_lhs(q, low_half):
    zero = jnp.zeros_like(q)
    return jnp.concatenate([jnp.where(low_half, q, zero), jnp.where(low_half, zero, q)], axis=0)


def _norm_proj_kernel(x_ref, g_ref, w_ref, o_ref, xn_ref, *, n_chunk):
    xn_ref[...] = _rms(x_ref[...], g_ref[...]).astype(BF16)
    for c in range(0, IN_WIDTH, n_chunk):
        o_ref[:, c:c + n_chunk] = jnp.dot(
            xn_ref[...], w_ref[:, c:c + n_chunk], preferred_element_type=F32).astype(BF16)


def _norm_proj(h, g, w, layer, *, tm=512, n_chunk=512):
    m = h.shape[0]
    blocks = 2 * tm * D_MODEL * 4 + D_MODEL * IN_WIDTH * 2 + 2 * tm * IN_WIDTH * 2 + tm * D_MODEL * 2
    return pl.pallas_call(
        functools.partial(_norm_proj_kernel, n_chunk=n_chunk),
        grid=(m // tm,),
        in_specs=[
            pl.BlockSpec((tm, D_MODEL), lambda i: (i, 0)),
            pl.BlockSpec((None, 1, D_MODEL), lambda i: (layer, 0, 0)),
            pl.BlockSpec((None, D_MODEL, IN_WIDTH), lambda i: (layer, 0, 0), pipeline_mode=pl.Buffered(1)),
        ],
        out_specs=pl.BlockSpec((tm, IN_WIDTH), lambda i: (i, 0)),
        out_shape=jax.ShapeDtypeStruct((m, IN_WIDTH), BF16),
        scratch_shapes=[pltpu.VMEM((tm, D_MODEL), BF16)],
        compiler_params=pltpu.CompilerParams(
            dimension_semantics=("arbitrary",), vmem_limit_bytes=_vmem_limit(blocks)),
        name="norm_proj",
    )(h, g, w)


def _window_attn_kernel(sink_ref, q_ref, k_ref, v_ref, kp_ref, kn_ref, vp_ref, vn_ref, bias_ref, o_ref,
                        kbuf, vbuf, *, tq, blocks_per_seq, layer):
    i = pl.program_id(0)
    nb = tq // BLOCK
    band = 3 * BLOCK
    kbuf[0:BLOCK] = kp_ref[...]
    kbuf[BLOCK:BLOCK + tq] = k_ref[...]
    kbuf[BLOCK + tq:] = kn_ref[...]
    vbuf[0:BLOCK] = vp_ref[...]
    vbuf[BLOCK:BLOCK + tq] = v_ref[...]
    vbuf[BLOCK + tq:] = vn_ref[...]
    col = lax.broadcasted_iota(jnp.int32, (1, band), 1)
    low_half = lax.broadcasted_iota(jnp.int32, (A_GROUP * BLOCK, LANES), 1) < HEAD_DIM
    low_half_blk = lax.broadcasted_iota(jnp.int32, (BLOCK, LANES), 1) < HEAD_DIM

    def body(j, carry):
        nseq = lax.rem(i * nb + j, blocks_per_seq)
        lo = jnp.where(nseq == 0, NEG, 0.0).astype(F32)
        hi = jnp.where(nseq == blocks_per_seq - 1, NEG, 0.0).astype(F32)
        pen = jnp.where(col < BLOCK, lo, jnp.where(col >= 2 * BLOCK, hi, 0.0))
        r0 = pl.multiple_of(j * BLOCK, BLOCK)
        for u in range(A_KV_TILES):
            lanes = slice(u * LANES, (u + 1) * LANES)
            kt = kbuf[pl.ds(r0, band), lanes]
            vt = vbuf[pl.ds(r0, band), lanes]
            q = jnp.concatenate(
                [q_ref[pl.ds(r0, BLOCK), (A_GROUP * u + g) * LANES:(A_GROUP * u + g + 1) * LANES]
                 for g in range(A_GROUP)], axis=0) * SCALE
            s = _nt_dot(_split_heads_lhs(q, low_half), kt) + bias_ref[u] + pen
            probs, inv = [], []
            for n in range(2 * A_GROUP):
                sn = s[n * BLOCK:(n + 1) * BLOCK]
                sink = sink_ref[layer, 2 * A_GROUP * u + n]
                m = jnp.maximum(jnp.max(sn, axis=-1, keepdims=True), sink)
                p = jnp.exp(sn - m)
                inv.append(1.0 / (jnp.sum(p, axis=-1, keepdims=True) + jnp.exp(sink - m)))
                probs.append(p.astype(BF16))
            o = jnp.dot(jnp.concatenate(probs, axis=0), vt, preferred_element_type=F32)
            for g in range(A_GROUP):
                o_lo = o[g * BLOCK:(g + 1) * BLOCK] * inv[g]
                o_hi = o[(A_GROUP + g) * BLOCK:(A_GROUP + g + 1) * BLOCK] * inv[A_GROUP + g]
                o_ref[pl.ds(r0, BLOCK), (A_GROUP * u + g) * LANES:(A_GROUP * u + g + 1) * LANES] = (
                    jnp.where(low_half_blk, o_lo, o_hi))
        return carry

    lax.fori_loop(0, nb, body, 0)


def _window_attn(proj, bias, sink, layer, seq_len, *, tq=512):
    m = proj.shape[0]
    nb = tq // BLOCK
    n_blocks = m // BLOCK
    ka, va = KA_OFF // KV_WIDTH, VA_OFF // KV_WIDTH
    blocks = 2 * (tq * A_WIDTH * 2 + 2 * (tq + 2 * BLOCK) * KV_WIDTH * 2 + tq * A_WIDTH * 4
                  + A_HEADS * BLOCK * 3 * BLOCK * 4) + 2 * (tq + 2 * BLOCK) * KV_WIDTH * 2
    prev_map = lambda c: (lambda i: (jnp.maximum(i * nb - 1, 0), c))
    next_map = lambda c: (lambda i: (jnp.minimum((i + 1) * nb, n_blocks - 1), c))
    return pl.pallas_call(
        functools.partial(_window_attn_kernel, tq=tq, blocks_per_seq=seq_len // BLOCK, layer=layer),
        grid=(m // tq,),
        in_specs=[
            pl.BlockSpec(memory_space=pltpu.SMEM),
            pl.BlockSpec((tq, A_WIDTH), lambda i: (i, QA_OFF // A_WIDTH)),
            pl.BlockSpec((tq, KV_WIDTH), lambda i: (i, ka)),
            pl.BlockSpec((tq, KV_WIDTH), lambda i: (i, va)),
            pl.BlockSpec((BLOCK, KV_WIDTH), prev_map(ka)),
            pl.BlockSpec((BLOCK, KV_WIDTH), next_map(ka)),
            pl.BlockSpec((BLOCK, KV_WIDTH), prev_map(va)),
            pl.BlockSpec((BLOCK, KV_WIDTH), next_map(va)),
            pl.BlockSpec((A_KV_TILES, 2 * A_GROUP * BLOCK, 3 * BLOCK), lambda i: (0, 0, 0)),
        ],
        out_specs=pl.BlockSpec((tq, A_WIDTH), lambda i: (i, 0)),
        out_shape=jax.ShapeDtypeStruct((m, A_WIDTH), F32),
        scratch_shapes=[pltpu.VMEM((tq + 2 * BLOCK, KV_WIDTH), BF16),
                        pltpu.VMEM((tq + 2 * BLOCK, KV_WIDTH), BF16)],
        compiler_params=pltpu.CompilerParams(
            dimension_semantics=("arbitrary",), vmem_limit_bytes=_vmem_limit(blocks)),
        name="window_attn",
    )(sink, proj, proj, proj, proj, proj, proj, proj, bias)


def _t5_bucket(rel):
    nb = N_BUCKETS // 2
    max_exact = nb // 2
    ret = (rel > 0).astype(jnp.int32) * nb
    n = jnp.abs(rel)
    nf = jnp.maximum(n, 1).astype(F32)
    large = max_exact + (jnp.log(nf / max_exact) / math.log(MAX_DIST / max_exact) * (nb - max_exact)).astype(jnp.int32)
    large = jnp.minimum(large, nb - 1)
    return ret + jnp.where(n < max_exact, n, large)


def _window_bias_table(rel_bias):
    rel = jnp.arange(3 * BLOCK)[None, :] - BLOCK - jnp.arange(BLOCK)[:, None]
    bias = rel_bias.astype(F32)[_t5_bucket(rel)]
    bias = jnp.where((jnp.abs(rel) <= WINDOW)[:, :, None], bias, NEG)
    return bias.transpose(2, 0, 1).reshape(A_KV_TILES, 2 * A_GROUP * BLOCK, 3 * BLOCK)


def _spatial_gate_kernel(u_ref, v_ref, lng_ref, lnb_ref, ws_ref, bs_ref, o_ref, vn_ref, *, tm):
    o_ref[...] = _gelu(u_ref[...].astype(F32))
    gv = _gelu(v_ref[...].astype(F32))
    mu = jnp.mean(gv, axis=-1, keepdims=True)
    d = gv - mu
    var = jnp.mean(d * d, axis=-1, keepdims=True)
    vn_ref[...] = (d * lax.rsqrt(var + EPS) * lng_ref[...] + lnb_ref[...]).astype(BF16)
    low_half = lax.broadcasted_iota(jnp.int32, (CHUNK, LANES), 1) < HEAD_DIM
    for c in range(tm // CHUNK):
        rows = slice(c * CHUNK, (c + 1) * CHUNK)
        for gp in range(B_GROUPS // 2):
            cols = slice(gp * LANES, (gp + 1) * LANES)
            vp = vn_ref[rows, cols]
            r0 = jnp.dot(ws_ref[2 * gp], vp, preferred_element_type=F32)
            r1 = jnp.dot(ws_ref[2 * gp + 1], vp, preferred_element_type=F32)
            s = jnp.where(low_half, r0, r1) + bs_ref[:, cols]
            o_ref[rows, cols] = o_ref[rows, cols] * s


def _spatial_gate(proj, ln_g, ln_b, w_s, bs_tab, layer, *, tm=512):
    m = proj.shape[0]
    blocks = 2 * (2 * tm * B_WIDTH * 2 + tm * B_WIDTH * 4) + tm * B_WIDTH * 2 + 4 * tm * B_WIDTH * 4
    return pl.pallas_call(
        functools.partial(_spatial_gate_kernel, tm=tm),
        grid=(m // tm,),
        in_specs=[
            pl.BlockSpec((tm, B_WIDTH), lambda i: (i, UB_OFF // B_WIDTH)),
            pl.BlockSpec((tm, B_WIDTH), lambda i: (i, VB_OFF // B_WIDTH)),
            pl.BlockSpec((None, 1, B_WIDTH), lambda i: (layer, 0, 0)),
            pl.BlockSpec((None, 1, B_WIDTH), lambda i: (layer, 0, 0)),
            pl.BlockSpec((None, B_GROUPS, CHUNK, CHUNK), lambda i: (layer, 0, 0, 0)),
            pl.BlockSpec((None, CHUNK, B_WIDTH), lambda i: (layer, 0, 0)),
        ],
        out_specs=pl.BlockSpec((tm, B_WIDTH), lambda i: (i, 0)),
        out_shape=jax.ShapeDtypeStruct((m, B_WIDTH), F32),
        scratch_shapes=[pltpu.VMEM((tm, B_WIDTH), BF16)],
        compiler_params=pltpu.CompilerParams(
            dimension_semantics=("arbitrary",), vmem_limit_bytes=_vmem_limit(blocks)),
        name="spatial_gate",
    )(proj, proj, ln_g, ln_b, w_s, bs_tab)


NA_KEYS = NA_ROWS * GRID_W
NA_BIAS_TILES = 2 * NA_ROWS - 2


def _nbr_attn_kernel(q_ref, k_ref, v_ref, bias_ref, o_ref, *, rows_per_step, rows):
    rb = pl.program_id(2)
    low_half = lax.broadcasted_iota(jnp.int32, (GRID_W, LANES), 1) < HEAD_DIM
    for j in range(rows_per_step):
        r = rb * rows_per_step + j
        row_start = jnp.clip(r - NA_ROWS // 2, 0, rows - NA_ROWS)
        delta = r - row_start
        k0 = pl.multiple_of(row_start * GRID_W, GRID_W)
        q_rows = slice(j * GRID_W, (j + 1) * GRID_W)
        kw = k_ref[pl.ds(k0, NA_KEYS), :]
        vw = v_ref[pl.ds(k0, NA_KEYS), :]
        s = _nt_dot(_split_heads_lhs(q_ref[q_rows, :] * SCALE, low_half), kw)
        s = s + jnp.concatenate(
            [bias_ref[2 * t - delta + NA_ROWS - 1] for t in range(NA_ROWS // 2)], axis=1)
        m = jnp.max(s, axis=-1, keepdims=True)
        p = jnp.exp(s - m)
        inv = 1.0 / jnp.sum(p, axis=-1, keepdims=True)
        o = jnp.dot(p.astype(BF16), vw, preferred_element_type=F32) * inv
        o_ref[q_rows, :] = jnp.where(low_half, o[:GRID_W], o[GRID_W:])


def _nbr_attn(proj, bias, layer, batch, seq_len, *, rows_per_step=8):
    m = proj.shape[0]
    tq = rows_per_step * GRID_W
    steps = seq_len // tq
    rows = seq_len // GRID_W
    assert rows >= NA_ROWS
    blocks = 2 * (tq * LANES * 2 + 2 * seq_len * LANES * 2 + NA_BIAS_TILES * 2 * GRID_W * LANES * 4
                  + tq * LANES * 4)
    return pl.pallas_call(
        functools.partial(_nbr_attn_kernel, rows_per_step=rows_per_step, rows=rows),
        grid=(batch, C_HEADS // 2, steps),
        in_specs=[
            pl.BlockSpec((tq, LANES), lambda b, p, r: (b * steps + r, QC_OFF // LANES + p)),
            pl.BlockSpec((seq_len, LANES), lambda b, p, r: (b, KC_OFF // LANES + p)),
            pl.BlockSpec((seq_len, LANES), lambda b, p, r: (b, VC_OFF // LANES + p)),
            pl.BlockSpec((None, None, NA_BIAS_TILES, 2 * GRID_W, LANES), lambda b, p, r: (layer, p, 0, 0, 0)),
        ],
        out_specs=pl.BlockSpec((tq, LANES), lambda b, p, r: (b * steps + r, p)),
        out_shape=jax.ShapeDtypeStruct((m, C_WIDTH), F32),
        compiler_params=pltpu.CompilerParams(
            dimension_semantics=("arbitrary", "arbitrary", "arbitrary"),
            vmem_limit_bytes=_vmem_limit(blocks)),
        name="nbr_attn",
    )(proj, proj, proj, bias)


def _nbr_bias_table(rpb):
    qc = np.arange(GRID_W)
    kc = np.arange(GRID_W)
    col_start = np.clip(qc - NA_COLS // 2, 0, GRID_W - NA_COLS)
    valid = (kc[None, :] >= col_start[:, None]) & (kc[None, :] < col_start[:, None] + NA_COLS)
    ci = np.clip(kc[None, :] - qc[:, None] + NA_COLS - 1, 0, 2 * NA_COLS - 2)
    per_row = jnp.take(rpb.astype(F32), jnp.asarray(ci), axis=2)
    per_row = jnp.where(valid[None, None], per_row, NEG)
    two_rows = jnp.concatenate([per_row[:, :-1], per_row[:, 1:]], axis=-1)
    two_rows = two_rows.reshape(C_HEADS // 2, 2, NA_BIAS_TILES, GRID_W, 2 * GRID_W)
    return two_rows.transpose(0, 2, 1, 3, 4).reshape(C_HEADS // 2, NA_BIAS_TILES, 2 * GRID_W, 2 * GRID_W)


def _merge_kernel(ya_ref, yb_ref, yc_ref, h_ref, ga_ref, gb_ref, gc_ref, wa_ref, wb_ref, wc_ref, o_ref):
    acc = h_ref[...]
    acc = acc + jnp.dot(_rms(ya_ref[...], ga_ref[...]).astype(BF16), wa_ref[...], preferred_element_type=F32)
    acc = acc + jnp.dot(_rms(yb_ref[...], gb_ref[...]).astype(BF16), wb_ref[...], preferred_element_type=F32)
    acc = acc + jnp.dot(_rms(yc_ref[...], gc_ref[...]).astype(BF16), wc_ref[...], preferred_element_type=F32)
    o_ref[...] = acc


def _merge(ya, yb, yc, h, ga, gb, gc, wa, wb, wc, layer, *, tm=256):
    m = h.shape[0]
    row = lambda w: pl.BlockSpec((tm, w), lambda i: (i, 0))
    const = lambda r, c: pl.BlockSpec((None, r, c), lambda i: (layer, 0, 0))
    blocks = 2 * tm * (A_WIDTH + B_WIDTH + C_WIDTH + 2 * D_MODEL) * 4 + 2 * D_MODEL * D_MODEL * 2
    return pl.pallas_call(
        _merge_kernel,
        grid=(m // tm,),
        in_specs=[row(A_WIDTH), row(B_WIDTH), row(C_WIDTH), row(D_MODEL),
                  const(1, A_WIDTH), const(1, B_WIDTH), const(1, C_WIDTH),
                  const(A_WIDTH, D_MODEL), const(B_WIDTH, D_MODEL), const(C_WIDTH, D_MODEL)],
        out_specs=row(D_MODEL),
        out_shape=jax.ShapeDtypeStruct((m, D_MODEL), F32),
        compiler_params=pltpu.CompilerParams(
            dimension_semantics=("arbitrary",), vmem_limit_bytes=_vmem_limit(blocks)),
        name="merge",
    )(ya, yb, yc, h, ga, gb, gc, wa, wb, wc)


def _conv3(hs_ref, cw, tm):
    return (hs_ref[HALO - 1:HALO - 1 + tm] * cw[0:1] + hs_ref[HALO:HALO + tm] * cw[1:2]
            + hs_ref[HALO + 1:HALO + 1 + tm] * cw[2:3] + cw[3:4])


def _conv_ffn_kernel(x_ref, xp_ref, xn_ref, g_ref, wg_ref, wv_ref, cwg_ref, cwv_ref, wdn_ref, fg_ref, o_ref,
                     xs_ref, hg_ref, hv_ref, *, tm, tiles_per_seq, final_norm):
    i = pl.program_id(0)
    j = pl.program_id(1)

    @pl.when(j == 0)
    def _():
        g = g_ref[...]
        tseq = lax.rem(i, tiles_per_seq)
        keep_prev = jnp.where(tseq == 0, 0.0, 1.0).astype(F32)
        keep_next = jnp.where(tseq == tiles_per_seq - 1, 0.0, 1.0).astype(F32)
        xs_ref[0:HALO] = (_rms(xp_ref[...], g) * keep_prev).astype(BF16)
        xs_ref[HALO:HALO + tm] = _rms(x_ref[...], g).astype(BF16)
        xs_ref[HALO + tm:] = (_rms(xn_ref[...], g) * keep_next).astype(BF16)
        o_ref[...] = x_ref[...]

    hg_ref[...] = jnp.dot(xs_ref[...], wg_ref[...], preferred_element_type=F32)
    hv_ref[...] = jnp.dot(xs_ref[...], wv_ref[...], preferred_element_type=F32)
    act = (_gelu(_conv3(hg_ref, cwg_ref[...], tm)) * _conv3(hv_ref, cwv_ref[...], tm)).astype(BF16)
    o_ref[...] += jnp.dot(act, wdn_ref[...], preferred_element_type=F32)

    if final_norm:
        @pl.when(j == N_FF_TILES - 1)
        def _():
            o_ref[...] = _rms(o_ref[...], fg_ref[...])


def _conv_ffn(h, g, w_gate, w_val, cw_gate, cw_val, w_down, final_g, layer, seq_len, *, final_norm, tm=512):
    m = h.shape[0]
    halo_blocks = tm // HALO
    n_halo = m // HALO
    blocks = (2 * tm * D_MODEL * 4 * 2 + 2 * (D_MODEL * 2 * FF_TILE * 2 + FF_TILE * D_MODEL * 2)
              + (tm + 2 * HALO) * (D_MODEL * 2 + 2 * FF_TILE * 4))
    return pl.pallas_call(
        functools.partial(_conv_ffn_kernel, tm=tm, tiles_per_seq=seq_len // tm, final_norm=final_norm),
        grid=(m // tm, N_FF_TILES),
        in_specs=[
            pl.BlockSpec((tm, D_MODEL), lambda i, j: (i, 0)),
            pl.BlockSpec((HALO, D_MODEL), lambda i, j: (jnp.maximum(i * halo_blocks - 1, 0), 0)),
            pl.BlockSpec((HALO, D_MODEL), lambda i, j: (jnp.minimum((i + 1) * halo_blocks, n_halo - 1), 0)),
            pl.BlockSpec((None, 1, D_MODEL), lambda i, j: (layer, 0, 0)),
            pl.BlockSpec((None, None, D_MODEL, FF_TILE), lambda i, j: (layer, j, 0, 0)),
            pl.BlockSpec((None, None, D_MODEL, FF_TILE), lambda i, j: (layer, j, 0, 0)),
            pl.BlockSpec((None, 8, FF_TILE), lambda i, j: (layer, 0, j)),
            pl.BlockSpec((None, 8, FF_TILE), lambda i, j: (layer, 0, j)),
            pl.BlockSpec((None, FF_TILE, D_MODEL), lambda i, j: (layer, j, 0)),
            pl.BlockSpec((1, D_MODEL), lambda i, j: (0, 0)),
        ],
        out_specs=pl.BlockSpec((tm, D_MODEL), lambda i, j: (i, 0)),
        out_shape=jax.ShapeDtypeStruct((m, D_MODEL), F32),
        scratch_shapes=[pltpu.VMEM((tm + 2 * HALO, D_MODEL), BF16),
                        pltpu.VMEM((tm + 2 * HALO, FF_TILE), F32),
                        pltpu.VMEM((tm + 2 * HALO, FF_TILE), F32)],
        compiler_params=pltpu.CompilerParams(
            dimension_semantics=("arbitrary", "arbitrary"), vmem_limit_bytes=_vmem_limit(blocks)),
        name="conv_ffn",
    )(h, h, h, g, w_gate, w_val, cw_gate, cw_val, w_down, final_g)


def _reorder_heads(a, axis, order):
    slabs = jnp.split(a, a.shape[axis] // HEAD_DIM, axis=axis)
    return jnp.concatenate([slabs[h] for h in order], axis=axis)


def _pad_ff(a):
    return jnp.pad(a, [(0, 0)] * (a.ndim - 1) + [(0, D_FF_PAD - D_FF)])


def _up_tiles(w):
    tiles = _pad_ff(w).astype(BF16).reshape(DEPTH, D_MODEL, N_FF_TILES, FF_TILE)
    return tiles.transpose(0, 2, 1, 3)


def _conv_rows(conv_w, conv_b):
    rows = jnp.concatenate([conv_w, conv_b[:, None, :]], axis=1)
    return jnp.pad(_pad_ff(rows), ((0, 0), (0, 8 - rows.shape[1]), (0, 0)))


def _prepare_params(rel_bias, norm1_g, w_in, sink, sgu_ln_g, sgu_ln_b, w_spatial, b_spatial, na_rpb,
                    gn_a, gn_b, gn_c, w_out, norm2_g, w_up, conv_w, conv_b, w_down, final_g):
    src = [int(o) for o in np.cumsum((A_WIDTH, KV_WIDTH, KV_WIDTH, B_WIDTH, B_WIDTH, C_WIDTH, C_WIDTH))]
    qa, ka, va, ub, vb, qc, kc, vc = jnp.split(w_in.astype(BF16), src, axis=-1)
    w_in_cols = jnp.concatenate([_reorder_heads(qa, 2, A_HEAD_ORDER), qc, kc, vc, ub, vb, ka, va], axis=-1)
    w_out_b = w_out.astype(BF16)
    return dict(
        norm1_g=norm1_g[:, None, :],
        w_in=w_in_cols,
        bias_a=_window_bias_table(rel_bias),
        sink=sink.astype(F32),
        ln_g=sgu_ln_g[:, None, :], ln_b=sgu_ln_b[:, None, :],
        w_s=w_spatial.astype(BF16),
        bs_tab=jnp.repeat(jnp.swapaxes(b_spatial, 1, 2), HEAD_DIM, axis=2).astype(F32),
        bias_c=jax.vmap(_nbr_bias_table)(na_rpb),
        gn_a=_reorder_heads(gn_a, 1, A_HEAD_ORDER)[:, None, :], gn_b=gn_b[:, None, :], gn_c=gn_c[:, None, :],
        wo_a=_reorder_heads(w_out_b[:, :A_WIDTH], 1, A_HEAD_ORDER),
        wo_b=w_out_b[:, A_WIDTH:A_WIDTH + B_WIDTH], wo_c=w_out_b[:, A_WIDTH + B_WIDTH:],
        norm2_g=norm2_g[:, None, :],
        w_gate=_up_tiles(w_up[:, :, :D_FF]), w_val=_up_tiles(w_up[:, :, D_FF:]),
        cw_gate=_conv_rows(conv_w[:, :, :D_FF], conv_b[:, :D_FF]),
        cw_val=_conv_rows(conv_w[:, :, D_FF:], conv_b[:, D_FF:]),
        w_down=jnp.pad(w_down, ((0, 0), (0, D_FF_PAD - D_FF), (0, 0))).astype(BF16),
        final_g=final_g[None, :],
    )


def _trunk(x, p):
    batch, seq_len, _ = x.shape
    h = x.reshape(batch * seq_len, D_MODEL)
    for l in range(DEPTH):
        proj = _norm_proj(h, p["norm1_g"], p["w_in"], l)
        ya = _window_attn(proj, p["bias_a"], p["sink"], l, seq_len)
        yb = _spatial_gate(proj, p["ln_g"], p["ln_b"], p["w_s"], p["bs_tab"], l)
        yc = _nbr_attn(proj, p["bias_c"], l, batch, seq_len)
        h = _merge(ya, yb, yc, h, p["gn_a"], p["gn_b"], p["gn_c"], p["wo_a"], p["wo_b"], p["wo_c"], l)
        h = _conv_ffn(h, p["norm2_g"], p["w_gate"], p["w_val"], p["cw_gate"], p["cw_val"], p["w_down"],
                      p["final_g"], l, seq_len, final_norm=(l == DEPTH - 1))
    return h.reshape(batch, seq_len, D_MODEL)


def kernel(x_prompt, x_sample, rel_bias, norm1_g, w_in, sink, sgu_ln_g, sgu_ln_b, w_spatial, b_spatial, na_rpb, gn_a, gn_b, gn_c, w_out, norm2_g, w_up, conv_w, conv_b, w_down, final_g):
    p = _prepare_params(rel_bias, norm1_g, w_in, sink, sgu_ln_g, sgu_ln_b, w_spatial, b_spatial, na_rpb,
                        gn_a, gn_b, gn_c, w_out, norm2_g, w_up, conv_w, conv_b, w_down, final_g)
    return (_trunk(x_prompt, p), _trunk(x_sample, p))
```
